```python
import jax, jax.numpy as jnp
from jax import lax
import numpy as np

D_MODEL = 1024
BATCH = 4
SEQ = 4096
DEPTH = 4
DEC_BATCH = 32
DEC_SEQ = 1
PAST_LEN = 8192
PAGE_SIZE = 128

N_A_LAYERS = DEPTH // 2
N_B_LAYERS = DEPTH - N_A_LAYERS
N_DENSE = (DEPTH + 1) // 2
N_MOE = DEPTH // 2
A_DK = 128
A_HEADS = D_MODEL // A_DK
A_DF = A_HEADS * A_DK
A_DV = D_MODEL // A_HEADS
A_DI = A_HEADS * A_DV
A_CHUNK = 16
B_HEAD_DIM = 64
B_HEADS = D_MODEL // B_HEAD_DIM
B_DIM = B_HEADS * B_HEAD_DIM
Q_BLOCK = 128
D_FF = ((8 * D_MODEL // 3 + 127) // 128) * 128
N_EXPERTS = 8
TOP_K = 2
D_FF_EXPERT = 7 * D_MODEL // 2
RMS_EPS = 1e-6

kernel_name = 'yoco_hgrn2_fox_hybrid_step'


def rmsnorm(x, w):
    xf = x.astype(jnp.float32)
    y = xf * lax.rsqrt(jnp.mean(xf * xf, axis=-1, keepdims=True) + RMS_EPS) * w.astype(jnp.float32)
    return y.astype(x.dtype)


def modulate(x, w, shift, scale):
    return rmsnorm(x, w) * (1 + scale[:, None, :]) + shift[:, None, :]


def cond_split(c, w, b, n):
    m = jax.nn.silu(c) @ w + b
    return jnp.split(m, n, axis=-1)


def gla_chunked(q, k, v, log_f, s0):
    b, t, h, dk = q.shape
    dv = v.shape[-1]
    pad = (-t) % A_CHUNK
    padw = ((0, 0), (0, pad), (0, 0), (0, 0))
    q, k, v, log_f = [jnp.pad(a, padw) for a in (q, k, v, log_f)]
    n = (t + pad) // A_CHUNK

    def blocks(a):
        return a.reshape(b, n, A_CHUNK, h, a.shape[-1]).transpose(1, 0, 3, 2, 4)

    q, k, v, log_f = [blocks(a) for a in (q, k, v, log_f)]
    g = jnp.cumsum(log_f, axis=3)
    g_last = g[:, :, :, -1:, :]
    q_dec = q * jnp.exp(g)
    k_inv = k * jnp.exp(-g)
    causal = jnp.tril(jnp.ones((A_CHUNK, A_CHUNK), dtype=bool))
    att = jnp.where(causal, jnp.einsum('nbhtk,nbhsk->nbhts', q_dec, k_inv), 0.0)
    o_intra = jnp.einsum('nbhts,nbhsv->nbhtv', att, v)
    k_end = k * jnp.exp(g_last - g)

    def step(s, inp):
        qd, ke, vc, gl = inp
        o = jnp.einsum('bhtk,bhkv->bhtv', qd, s)
        s = jnp.exp(gl[:, :, 0, :])[..., None] * s + jnp.einsum('bhsk,bhsv->bhkv', ke, vc)
        return s, o

    s_final, o_inter = lax.scan(step, s0, (q_dec, k_end, v, g_last))
    o = (o_intra + o_inter).transpose(1, 0, 3, 2, 4).reshape(b, n * A_CHUNK, h, dv)[:, :t]
    return o, s_final


def hgrn2_mixer(h, w_in, lb, g_norm_w, w_o, s0):
    b, t, _ = h.shape
    proj = h @ w_in
    zq, zf, zi, zg = jnp.split(proj, [A_DF, 2 * A_DF, 2 * A_DF + A_DI], axis=-1)
    q = jax.nn.silu(zq.astype(jnp.float32))
    f = lb + (1 - lb) * jax.nn.sigmoid(zf.astype(jnp.float32))
    k = 1 - f
    o, s = gla_chunked(q.reshape(b, t, A_HEADS, A_DK), k.reshape(b, t, A_HEADS, A_DK),
                       zi.astype(jnp.float32).reshape(b, t, A_HEADS, A_DV),
                       jnp.log(f).reshape(b, t, A_HEADS, A_DK), s0.astype(jnp.float32))
    o = rmsnorm(o, g_norm_w) * jax.nn.silu(zg.astype(jnp.float32).reshape(b, t, A_HEADS, A_DV))
    return o.reshape(b, t, A_DI).astype(h.dtype) @ w_o, s


def shared_kv(x, c, kv_norm, kv_ada_w, kv_ada_b, kv_w, kv_bf):
    b, t, _ = x.shape
    shift, scale = cond_split(c, kv_ada_w, kv_ada_b, 2)
    hk = modulate(x, kv_norm, shift, scale)
    k, v, zf = jnp.split(hk @ kv_w, [B_DIM, 2 * B_DIM], axis=-1)
    logf = jax.nn.log_sigmoid(zf.astype(jnp.float32) + kv_bf.astype(jnp.float32))
    return (k.reshape(b, t, B_HEADS, B_HEAD_DIM), v.reshape(b, t, B_HEADS, B_HEAD_DIM), logf)


def fox_prompt(q, k, v, logf):
    b, t, h, d = q.shape
    nb = t // Q_BLOCK
    scale = d ** -0.5
    cum = jnp.cumsum(logf.astype(jnp.float32), axis=1)
    qb = q.reshape(b, nb, Q_BLOCK, h, d).swapaxes(0, 1)
    fq = cum.reshape(b, nb, Q_BLOCK, h).swapaxes(0, 1)
    fk = cum.transpose(0, 2, 1)[:, :, None, :]
    pos_k = jnp.arange(t)

    def block(args):
        qi, fqi, i = args
        s = jnp.einsum('bqhd,bkhd->bhqk', qi, k, preferred_element_type=jnp.float32) * scale
        s = s + fqi.transpose(0, 2, 1)[..., None] - fk
        pos_q = i * Q_BLOCK + jnp.arange(Q_BLOCK)
        s = jnp.where(pos_k[None, :] <= pos_q[:, None], s, -jnp.inf)
        p = jax.nn.softmax(s, axis=-1)
        return jnp.einsum('bhqk,bkhd->bqhd', p.astype(v.dtype), v)

    o = lax.map(block, (qb, fq, jnp.arange(nb)))
    return o.swapaxes(0, 1).reshape(b, t, h, d)


def fox_sample(q, k_new, v_new, logf_new, k_past, v_past, logf_past):
    tn = q.shape[1]
    lp = k_past.shape[1]
    scale = q.shape[-1] ** -0.5
    cum = jnp.cumsum(jnp.concatenate([logf_past.astype(jnp.float32), logf_new.astype(jnp.float32)], axis=1), axis=1)
    f_past = cum[:, :lp].transpose(0, 2, 1)[:, :, None, :]
    f_new = cum[:, lp:].transpose(0, 2, 1)
    fq = f_new[..., None]
    s_past = jnp.einsum('bqhd,bkhd->bhqk', q, k_past, preferred_element_type=jnp.float32) * scale + fq - f_past
    s_new = jnp.einsum('bqhd,bkhd->bhqk', q, k_new, preferred_element_type=jnp.float32) * scale + fq - f_new[:, :, None, :]
    s_new = jnp.where(jnp.tril(jnp.ones((tn, tn), dtype=bool)), s_new, -jnp.inf)
    p = jax.nn.softmax(jnp.concatenate([s_past, s_new], axis=-1), axis=-1)
    return (jnp.einsum('bhqk,bkhd->bqhd', p[..., :lp].astype(v_past.dtype), v_past)
            + jnp.einsum('bhqk,bkhd->bqhd', p[..., lp:].astype(v_new.dtype), v_new))


def swiglu(h, w_gu, w_d):
    g, u = jnp.split(h @ w_gu, 2, axis=-1)
    return (jax.nn.silu(g) * u) @ w_d


def moe_swiglu(h, w_r, b_r, w_gu, w_d):
    logits = (h @ w_r).astype(jnp.float32) + b_r.astype(jnp.float32)
    top_v, top_i = lax.top_k(logits, TOP_K)
    gates = jax.nn.softmax(top_v, axis=-1)
    gate_full = jnp.sum(jax.nn.one_hot(top_i, N_EXPERTS, dtype=jnp.float32) * gates[..., None], axis=-2)
    out = jnp.zeros_like(h)
    for e in range(N_EXPERTS):
        out = out + gate_full[..., e:e + 1].astype(h.dtype) * swiglu(h, w_gu[e], w_d[e])
    return out


def trunk(x, c, s0_a, past, p):
    b, t, _ = x.shape
    lb_p = jax.nn.softmax(p['a_lb'].astype(jnp.float32), axis=0)
    lower_bounds = jnp.cumsum(lb_p, axis=0) - lb_p[0]
    new_states = []
    kv = None
    for layer in range(DEPTH):
        sh1, sc1, g1, sh2, sc2, g2 = cond_split(c, p['ada_w'][layer], p['ada_b'][layer], 6)
        if layer == N_A_LAYERS:
            kv = shared_kv(x, c, p['kv_norm'], p['kv_ada_w'], p['kv_ada_b'], p['kv_w'], p['kv_bf'])
        h = modulate(x, p['norm_mix'][layer], sh1, sc1)
        if layer < N_A_LAYERS:
            o, s = hgrn2_mixer(h, p['a_w_in'][layer], lower_bounds[layer], p['a_gnorm'][layer],
                               p['a_w_o'][layer], s0_a[layer])
            new_states.append(s)
        else:
            j = layer - N_A_LAYERS
            q = (h @ p['b_w_q'][j]).reshape(b, t, B_HEADS, B_HEAD_DIM)
            if past is None:
                o = fox_prompt(q, kv[0], kv[1], kv[2])
            else:
                o = fox_sample(q, kv[0], kv[1], kv[2], past[0], past[1], past[2])
            o = o.reshape(b, t, B_DIM) @ p['b_w_o'][j]
        x = x + g1[:, None, :] * o
        h = modulate(x, p['norm_ffn'][layer], sh2, sc2)
        if layer % 2 == 0:
            f = swiglu(h, p['ffn_w_gu'][layer // 2], p['ffn_w_d'][layer // 2])
        else:
            i = layer // 2
            f = moe_swiglu(h, p['moe_w_r'][i], p['moe_b_r'][i], p['moe_w_gu'][i], p['moe_w_d'][i])
        x = x + g2[:, None, :] * f
    shf, scf = cond_split(c, p['out_ada_w'], p['out_ada_b'], 2)
    y = modulate(x, p['final_norm'], shf, scf)
    return y, jnp.stack(new_states), kv


def setup_inputs(seed: int = 0) -> dict:
    key = jax.random.key(seed)
    ks = iter(jax.random.split(key, 48))

    def nrm(shape, s):
        return jax.random.normal(next(ks), shape, jnp.float32) * s

    D = D_MODEL
    n_pages = PAST_LEN // PAGE_SIZE
    n_used = DEC_BATCH * n_pages
    n_pool = n_used + n_used // 4
    x_prompt = nrm((BATCH, SEQ, D), 1.0)
    x_sample = nrm((DEC_BATCH, DEC_SEQ, D), 1.0)
    c_prompt = nrm((BATCH, D), 1.0)
    c_sample = nrm((DEC_BATCH, D), 1.0)
    cache_k = nrm((n_pool, PAGE_SIZE, B_HEADS, B_HEAD_DIM), 1.0)
    cache_v = nrm((n_pool, PAGE_SIZE, B_HEADS, B_HEAD_DIM), 1.0)
    cache_logf = jax.nn.log_sigmoid(3.0 + nrm((n_pool, PAGE_SIZE, B_HEADS), 1.0))
    state_a = nrm((N_A_LAYERS, DEC_BATCH, A_HEADS, A_DK, A_DV), 0.3)
    page_table = jax.random.permutation(next(ks), n_pool)[:n_used].reshape(DEC_BATCH, n_pages).astype(jnp.int32)
    a_w_in = nrm((N_A_LAYERS, D, 2 * A_DF + 2 * A_DI), D ** -0.5)
    a_lb = nrm((N_A_LAYERS, A_DF), 0.1)
    a_gnorm = 1.0 + nrm((N_A_LAYERS, A_DV), 0.02)
    a_w_o = nrm((N_A_LAYERS, A_DI, D), A_DI ** -0.5)
    b_w_q = nrm((N_B_LAYERS, D, B_DIM), D ** -0.5)
    b_w_o = nrm((N_B_LAYERS, B_DIM, D), B_DIM ** -0.5)
    kv_norm = 1.0 + nrm((D,), 0.02)
    kv_ada_w = nrm((D, 2 * D), 0.5 * D ** -0.5)
    kv_ada_b = nrm((2 * D,), 0.02)
    kv_w = nrm((D, 2 * B_DIM + B_HEADS), D ** -0.5)
    kv_bf = 3.0 + nrm((B_HEADS,), 0.1)
    norm_mix = 1.0 + nrm((DEPTH, D), 0.02)
    norm_ffn = 1.0 + nrm((DEPTH, D), 0.02)
    ada_w = nrm((DEPTH, D, 6 * D), 0.5 * D ** -0.5)
    ada_b = nrm((DEPTH, 6 * D), 0.02)
    ffn_w_gu = nrm((N_DENSE, D, 2 * D_FF), D ** -0.5)
    ffn_w_d = nrm((N_DENSE, D_FF, D), D_FF ** -0.5)
    moe_w_r = nrm((N_MOE, D, N_EXPERTS), D ** -0.5)
    moe_b_r = nrm((N_MOE, N_EXPERTS), 0.01)
    moe_w_gu = nrm((N_MOE, N_EXPERTS, D, 2 * D_FF_EXPERT), D ** -0.5)
    moe_w_d = nrm((N_MOE, N_EXPERTS, D_FF_EXPERT, D), D_FF_EXPERT ** -0.5)
    final_norm = 1.0 + nrm((D,), 0.02)
    out_ada_w = nrm((D, 2 * D), 0.5 * D ** -0.5)
    out_ada_b = nrm((2 * D,), 0.02)
    return {'x_prompt': x_prompt, 'x_sample': x_sample, 'c_prompt': c_prompt, 'c_sample': c_sample,
            'cache_k': cache_k, 'cache_v': cache_v, 'cache_logf': cache_logf, 'state_a': state_a,
            'page_table': page_table,
            'a_w_in': a_w_in, 'a_lb': a_lb, 'a_gnorm': a_gnorm, 'a_w_o': a_w_o,
            'b_w_q': b_w_q, 'b_w_o': b_w_o,
            'kv_norm': kv_norm, 'kv_ada_w': kv_ada_w, 'kv_ada_b': kv_ada_b, 'kv_w': kv_w, 'kv_bf': kv_bf,
            'norm_mix': norm_mix, 'norm_ffn': norm_ffn, 'ada_w': ada_w, 'ada_b': ada_b,
            'ffn_w_gu': ffn_w_gu, 'ffn_w_d': ffn_w_d,
            'moe_w_r': moe_w_r, 'moe_b_r': moe_b_r, 'moe_w_gu': moe_w_gu, 'moe_w_d': moe_w_d,
            'final_norm': final_norm, 'out_ada_w': out_ada_w, 'out_ada_b': out_ada_b}


def reference(x_prompt, x_sample, c_prompt, c_sample, cache_k, cache_v, cache_logf, state_a, page_table,
              a_w_in, a_lb, a_gnorm, a_w_o, b_w_q, b_w_o,
              kv_norm, kv_ada_w, kv_ada_b, kv_w, kv_bf,
              norm_mix, norm_ffn, ada_w, ada_b, ffn_w_gu, ffn_w_d,
              moe_w_r, moe_b_r, moe_w_gu, moe_w_d, final_norm, out_ada_w, out_ada_b):
    params = dict(a_w_in=a_w_in, a_lb=a_lb, a_gnorm=a_gnorm, a_w_o=a_w_o, b_w_q=b_w_q, b_w_o=b_w_o,
                  kv_norm=kv_norm, kv_ada_w=kv_ada_w, kv_ada_b=kv_ada_b, kv_w=kv_w, kv_bf=kv_bf,
                  norm_mix=norm_mix, norm_ffn=norm_ffn, ada_w=ada_w, ada_b=ada_b,
                  ffn_w_gu=ffn_w_gu, ffn_w_d=ffn_w_d, moe_w_r=moe_w_r, moe_b_r=moe_b_r,
                  moe_w_gu=moe_w_gu, moe_w_d=moe_w_d, final_norm=final_norm,
                  out_ada_w=out_ada_w, out_ada_b=out_ada_b)
    db, n_pages = page_table.shape
    past_len = n_pages * cache_k.shape[1]
    k_past = cache_k[page_table].reshape(db, past_len, B_HEADS, B_HEAD_DIM)
    v_past = cache_v[page_table].reshape(db, past_len, B_HEADS, B_HEAD_DIM)
    logf_past = cache_logf[page_table].reshape(db, past_len, B_HEADS)
    s0_prompt = jnp.zeros((N_A_LAYERS, x_prompt.shape[0], A_HEADS, A_DK, A_DV), jnp.float32)
    y_prompt, state_a_prompt, kv_p = trunk(x_prompt, c_prompt, s0_prompt, None, params)
    y_sample, state_a_sample, kv_s = trunk(x_sample, c_sample, state_a, (k_past, v_past, logf_past), params)
    k_prompt, v_prompt, logf_prompt = kv_p
    k_sample, v_sample, logf_sample = kv_s
    return (y_prompt, y_sample, state_a_prompt, k_prompt, v_prompt, logf_prompt,
            state_a_sample, k_sample, v_sample, logf_sample)
```

```python
import functools

import jax
import jax.numpy as jnp
from jax import lax
from jax.experimental import pallas as pl
from jax.experimental.pallas import tpu as pltpu

F32 = jnp.float32
BF16 = jnp.bfloat16
RMS_EPS = 1e-6
LANES = 128
VMEM_LIMIT_BYTES = 56 * 2**20
GLA_SUB = 32

_NN = (((1,), (0,)), ((), ()))
_NT = (((1,), (1,)), ((), ()))


def _params(*sem):
    return pltpu.CompilerParams(dimension_semantics=sem, vmem_limit_bytes=VMEM_LIMIT_BYTES)


def _dg(a, b, dims):
    return lax.dot_general(a, b, dims, preferred_element_type=F32)


def _split2(x):
    hi = x.astype(BF16)
    return hi, (x - hi.astype(F32)).astype(BF16)


def _split3(x):
    hi = x.astype(BF16)
    r = x - hi.astype(F32)
    mid = r.astype(BF16)
    return hi, mid, (r - mid.astype(F32)).astype(BF16)


def _mm(a, b, dims=_NN, precise=False):
    if not precise:
        return _dg(a.astype(BF16), b.astype(BF16), dims)
    ah, al = _split2(a)
    bh, bl = _split2(b)
    return _dg(ah, bh, dims) + (_dg(ah, bl, dims) + _dg(al, bh, dims))


def _mm_exact_lhs(a_bf16, b, dims=_NN):
    hi, mid, lo = _split3(b)
    return _dg(a_bf16, hi, dims) + (_dg(a_bf16, mid, dims) + _dg(a_bf16, lo, dims))


def _silu(x):
    return x * jax.nn.sigmoid(x)


def _norm_mod(x, nw, shift, scale):
    ms = jnp.mean(x * x, axis=-1, keepdims=True)
    return (x * lax.rsqrt(ms + RMS_EPS) * nw) * (1.0 + scale) + shift


def _mod_spec(mod, tm, rows_per_group, tn=None):
    _, r, d = mod.shape
    if tn is None:
        return pl.BlockSpec((None, r, d), lambda i, j: ((i * tm) // rows_per_group, 0, 0))
    return pl.BlockSpec((None, r, tn), lambda i, j: ((i * tm) // rows_per_group, 0, j))


def _cond_kernel(c_ref, w_ref, b_ref, o_ref):
    o_ref[...] = _mm(_silu(c_ref[...]), w_ref[...], precise=True) + b_ref[...]


def cond_linear(c, w, b, tn=512):
    n_l, k, n = w.shape
    mc = c.shape[0]
    tn = min(tn, n)
    return pl.pallas_call(
        _cond_kernel,
        out_shape=jax.ShapeDtypeStruct((n_l, mc, n), F32),
        grid=(n_l, n // tn),
        in_specs=[pl.BlockSpec((mc, k), lambda l, j: (0, 0)),
                  pl.BlockSpec((None, k, tn), lambda l, j: (l, 0, j)),
                  pl.BlockSpec((None, 1, tn), lambda l, j: (l, 0, j))],
        out_specs=pl.BlockSpec((None, mc, tn), lambda l, j: (l, 0, j)),
        compiler_params=_params("arbitrary", "arbitrary"),
        name="cond_linear",
    )(c, w, b.reshape(n_l, 1, n))


def _nm_kernel(*refs, n_w, n_extra, n_out, precise, epilogue):
    x_ref, nw_ref, sh_ref, sc_ref = refs[:4]
    w_refs = refs[4:4 + n_w]
    extra_refs = refs[4 + n_w:4 + n_w + n_extra]
    out_refs = refs[4 + n_w + n_extra:4 + n_w + n_extra + n_out]
    h_ref = refs[-1]
    j = pl.program_id(1)

    @pl.when(j == 0)
    def _():
        h = _norm_mod(x_ref[...], nw_ref[...], sh_ref[...], sc_ref[...])
        h_ref[...] = h.astype(h_ref.dtype)

    epilogue(j, h_ref, w_refs, extra_refs, out_refs, precise)


def norm_mod_matmul(x, nw, shift, scale, w_specs, w_args, extra_specs, extra_args, out_shapes, out_specs,
                    epilogue, *, tm, n_steps, rows_per_group, precise, name, h_f32=False):
    m, d = x.shape
    kern = functools.partial(_nm_kernel, n_w=len(w_args), n_extra=len(extra_args), n_out=len(out_shapes),
                             precise=precise, epilogue=epilogue)
    return pl.pallas_call(
        kern,
        out_shape=out_shapes,
        grid=(m // tm, n_steps),
        in_specs=[pl.BlockSpec((tm, d), lambda i, j: (i, 0)),
                  pl.BlockSpec((1, d), lambda i, j: (0, 0)),
                  _mod_spec(shift, tm, rows_per_group),
                  _mod_spec(scale, tm, rows_per_group)] + list(w_specs) + list(extra_specs),
        out_specs=out_specs,
        scratch_shapes=[pltpu.VMEM((tm, d), F32 if (precise or h_f32) else BF16)],
        compiler_params=_params("arbitrary", "arbitrary"),
        name=name,
    )(x, nw.reshape(1, d), shift, scale, *w_args, *extra_args)


def _ep_swiglu(j, h_ref, w_refs, extra_refs, out_refs, precise):
    h = h_ref[...]
    g = _mm(h, w_refs[0][...], precise=precise)
    u = _mm(h, w_refs[1][...], precise=precise)
    out_refs[0][...] = (_silu(g) * u).astype(out_refs[0].dtype)


def nm_swiglu(x, nw, shift, scale, w_gu, *, tm, tn, rows_per_group, precise, out_dtype):
    m, d = x.shape
    f = w_gu.shape[1] // 2
    nj = f // tn
    return norm_mod_matmul(
        x, nw, shift, scale,
        [pl.BlockSpec((d, tn), lambda i, j: (0, j)), pl.BlockSpec((d, tn), lambda i, j: (0, j + nj))],
        [w_gu, w_gu], [], [],
        [jax.ShapeDtypeStruct((m, f), out_dtype)], [pl.BlockSpec((tm, tn), lambda i, j: (i, j))],
        _ep_swiglu, tm=tm, n_steps=nj, rows_per_group=rows_per_group, precise=precise, name="nm_swiglu")[0]


def _ep_scaled(j, h_ref, w_refs, extra_refs, out_refs, precise, *, out_scale):
    acc = _mm(h_ref[...], w_refs[0][...], precise=precise)
    out_refs[0][...] = (acc * out_scale).astype(out_refs[0].dtype)


def nm_linear(x, nw, shift, scale, w, *, out_scale, tm, tn, rows_per_group, precise, out_dtype):
    m, d = x.shape
    n = w.shape[1]
    return norm_mod_matmul(
        x, nw, shift, scale,
        [pl.BlockSpec((d, tn), lambda i, j: (0, j))], [w], [], [],
        [jax.ShapeDtypeStruct((m, n), out_dtype)], [pl.BlockSpec((tm, tn), lambda i, j: (i, j))],
        functools.partial(_ep_scaled, out_scale=out_scale),
        tm=tm, n_steps=n // tn, rows_per_group=rows_per_group, precise=precise, name="nm_linear")[0]


def _ep_hgrn2(j, h_ref, w_refs, extra_refs, out_refs, precise, *, tiles_per_part, layer):
    acc = _mm(h_ref[...], w_refs[0][...], precise=precise)
    o_ref = out_refs[0]
    part = j // tiles_per_part

    @pl.when(jnp.logical_or(part == 0, part == 3))
    def _():
        o_ref[...] = _silu(acc)

    @pl.when(part == 1)
    def _():
        a = extra_refs[0][...]
        e = jnp.exp(a - jnp.max(a, axis=0, keepdims=True))
        p = e / jnp.sum(e, axis=0, keepdims=True)
        lb = jnp.sum(p[:layer + 1], axis=0, keepdims=True) - p[0:1]
        f = lb + (1.0 - lb) * jax.nn.sigmoid(acc)
        o_ref[...] = jnp.log(f)

    @pl.when(part == 2)
    def _():
        o_ref[...] = acc


def nm_hgrn2_proj(x, nw, shift, scale, w_in, a_lb, layer, *, tm, tn, rows_per_group, precise):
    m, d = x.shape
    n = w_in.shape[1]
    df = n // 4
    tpp = df // tn
    n_l = a_lb.shape[0]
    return norm_mod_matmul(
        x, nw, shift, scale,
        [pl.BlockSpec((d, tn), lambda i, j: (0, j))], [w_in],
        [pl.BlockSpec((n_l, tn), lambda i, j: (0, j % tpp))], [a_lb],
        [jax.ShapeDtypeStruct((m, n), F32)], [pl.BlockSpec((tm, tn), lambda i, j: (i, j))],
        functools.partial(_ep_hgrn2, tiles_per_part=tpp, layer=layer),
        tm=tm, n_steps=n // tn, rows_per_group=rows_per_group, precise=precise, name="nm_hgrn2_proj")[0]


def _ep_kv(j, h_ref, w_refs, extra_refs, out_refs, precise):
    h = h_ref[...]
    k = _mm(h, w_refs[0][...], precise=precise)
    v = _mm(h, w_refs[1][...], precise=precise)
    out_refs[0][...] = k
    out_refs[1][...] = v
    out_refs[2][...] = k.astype(BF16)
    out_refs[3][...] = v.astype(BF16)

    @pl.when(j == 0)
    def _():
        zf = _mm(h, w_refs[2][...], precise=precise) + extra_refs[0][...]
        out_refs[4][...] = jax.nn.log_sigmoid(zf)


def nm_shared_kv(x, nw, shift, scale, w_k, w_v, w_f, b_f, *, tm, tn, rows_per_group, precise):
    m, d = x.shape
    n = w_k.shape[1]
    col = lambda i, j: (0, j)
    tile = pl.BlockSpec((tm, tn), lambda i, j: (i, j))
    return norm_mod_matmul(
        x, nw, shift, scale,
        [pl.BlockSpec((d, tn), col), pl.BlockSpec((d, tn), col), pl.BlockSpec((d, LANES), lambda i, j: (0, 0))],
        [w_k, w_v, w_f],
        [pl.BlockSpec((1, LANES), lambda i, j: (0, 0))], [b_f],
        [jax.ShapeDtypeStruct((m, n), F32), jax.ShapeDtypeStruct((m, n), F32),
         jax.ShapeDtypeStruct((m, n), BF16), jax.ShapeDtypeStruct((m, n), BF16),
         jax.ShapeDtypeStruct((m, LANES), F32)],
        [tile, tile, tile, tile, pl.BlockSpec((tm, LANES), lambda i, j: (i, 0))],
        _ep_kv, tm=tm, n_steps=n // tn, rows_per_group=rows_per_group, precise=precise, name="nm_shared_kv")


def _ep_router(j, h_ref, w_refs, extra_refs, out_refs, precise, *, n_experts):
    h = h_ref[...]
    logits = _mm(h, w_refs[0][...], precise=True) + extra_refs[0][...]
    lane_i = lax.broadcasted_iota(jnp.int32, logits.shape, 1)
    lane = lane_i.astype(F32)
    neg = jnp.float32(-jnp.inf)
    logits = jnp.where(lane_i < n_experts, logits, neg)
    v1 = jnp.max(logits, axis=-1, keepdims=True)
    i1 = jnp.min(jnp.where(logits == v1, lane, float(LANES)), axis=-1, keepdims=True)
    rest = jnp.where(lane == i1, neg, logits)
    v2 = jnp.max(rest, axis=-1, keepdims=True)
    i2 = jnp.min(jnp.where(rest == v2, lane, float(LANES)), axis=-1, keepdims=True)
    e2 = jnp.exp(v2 - v1)
    g1 = 1.0 / (1.0 + e2)
    g2 = e2 / (1.0 + e2)
    out_refs[0][...] = h.astype(out_refs[0].dtype)
    out_refs[1][...] = jnp.where(lane == i1, g1, jnp.where(lane == i2, g2, 0.0))
    out_refs[2][...] = jnp.where(lane_i == 0, i1, jnp.where(lane_i == 1, i2, 0.0)).astype(jnp.int32)


def nm_router(x, nw, shift, scale, w_r, b_r, *, tm, rows_per_group, precise, h_dtype):
    m, d = x.shape
    n_e = w_r.shape[1]
    w_pad = jnp.pad(w_r, ((0, 0), (0, LANES - n_e)))
    b_pad = jnp.pad(b_r, (0, LANES - n_e)).reshape(1, LANES)
    row = lambda i, j: (i, 0)
    return norm_mod_matmul(
        x, nw, shift, scale,
        [pl.BlockSpec((d, LANES), lambda i, j: (0, 0))], [w_pad],
        [pl.BlockSpec((1, LANES), lambda i, j: (0, 0))], [b_pad],
        [jax.ShapeDtypeStruct((m, d), h_dtype), jax.ShapeDtypeStruct((m, LANES), F32),
         jax.ShapeDtypeStruct((m, LANES), jnp.int32)],
        [pl.BlockSpec((tm, d), row), pl.BlockSpec((tm, LANES), row), pl.BlockSpec((tm, LANES), row)],
        functools.partial(_ep_router, n_experts=n_e),
        tm=tm, n_steps=1, rows_per_group=rows_per_group, precise=precise, name="nm_router", h_f32=True)


def _final_kernel(x_ref, nw_ref, sh_ref, sc_ref, o_ref):
    o_ref[...] = _norm_mod(x_ref[...], nw_ref[...], sh_ref[...], sc_ref[...])


def final_norm_mod(x, nw, shift, scale, *, tm, rows_per_group):
    m, d = x.shape
    return pl.pallas_call(
        _final_kernel,
        out_shape=jax.ShapeDtypeStruct((m, d), F32),
        grid=(m // tm, 1),
        in_specs=[pl.BlockSpec((tm, d), lambda i, j: (i, 0)),
                  pl.BlockSpec((1, d), lambda i, j: (0, 0)),
                  _mod_spec(shift, tm, rows_per_group), _mod_spec(scale, tm, rows_per_group)],
        out_specs=pl.BlockSpec((tm, d), lambda i, j: (i, 0)),
        compiler_params=_params("arbitrary", "arbitrary"),
        name="final_norm_mod",
    )(x, nw.reshape(1, d), shift, scale)


def _res_kernel(a_ref, w_ref, x_ref, g_ref, o_ref, *, precise):
    o_ref[...] = x_ref[...] + g_ref[...] * _mm(a_ref[...], w_ref[...], precise=precise)


def matmul_residual(a, w, x, gate, *, tm, tn, rows_per_group, precise):
    m, k = a.shape
    n = w.shape[1]
    return pl.pallas_call(
        functools.partial(_res_kernel, precise=precise),
        out_shape=jax.ShapeDtypeStruct((m, n), F32),
        grid=(m // tm, n // tn),
        in_specs=[pl.BlockSpec((tm, k), lambda i, j: (i, 0)),
                  pl.BlockSpec((k, tn), lambda i, j: (0, j)),
                  pl.BlockSpec((tm, tn), lambda i, j: (i, j)),
                  _mod_spec(gate, tm, rows_per_group, tn)],
        out_specs=pl.BlockSpec((tm, tn), lambda i, j: (i, j)),
        compiler_params=_params("arbitrary", "arbitrary"),
        name="matmul_residual",
    )(a, w, x, gate)


def _gla_kernel(q_ref, lf_ref, v_ref, gt_ref, gn_ref, s0_ref, o_ref, s_out_ref, st_ref, *, chunk, n_chunks):
    c = chunk
    tb = pl.program_id(2)

    @pl.when(tb == 0)
    def _():
        st_ref[...] = s0_ref[...].T

    ri = lax.broadcasted_iota(jnp.int32, (c, c), 0)
    ci = lax.broadcasted_iota(jnp.int32, (c, c), 1)
    tri = (ci <= ri).astype(BF16)
    row = lax.broadcasted_iota(jnp.int32, (c, 1), 0)
    sub_shift = GLA_SUB.bit_length() - 1
    mask_diag = jnp.logical_and((ri >> sub_shift) == (ci >> sub_shift), ci <= ri)
    levels = []
    blk = 2 * GLA_SUB
    while blk <= c:
        sh = blk.bit_length() - 1
        hi_half = (row & (blk - 1)) >= blk // 2
        same = None if blk == c else (ri >> sh) == (ci >> sh)
        levels.append((blk, hi_half, same))
        blk *= 2
    gn = gn_ref[...]

    def block_ref(g, blk_size, offset):
        rows = [jnp.broadcast_to(g[b * blk_size + offset:b * blk_size + offset + 1, :], (blk_size, g.shape[1]))
                for b in range(c // blk_size)]
        return rows[0] if len(rows) == 1 else jnp.concatenate(rows, axis=0)

    def body(ch, carry):
        r = pl.ds(pl.multiple_of(ch * c, c), c)
        q = q_ref[r, :]
        lf = lf_ref[r, :]
        v = v_ref[r, :]
        kk = 1.0 - jnp.exp(lf)
        g = _mm_exact_lhs(tri, lf)
        g_last = g[c - 1:c, :]
        st = st_ref[...]
        o = _dg((q * jnp.exp(g)).astype(BF16), st.astype(BF16), _NT)
        ref_d = block_ref(g, GLA_SUB, GLA_SUB // 2 - 1)
        p = _dg((q * jnp.exp(g - ref_d)).astype(BF16), (kk * jnp.exp(ref_d - g)).astype(BF16), _NT)
        att = jnp.where(mask_diag, p, 0.0)
        for blk_size, hi_half, same in levels:
            ref_l = block_ref(g, blk_size, blk_size // 2 - 1)
            ql = jnp.where(hi_half, q * jnp.exp(jnp.where(hi_half, g - ref_l, 0.0)), 0.0)
            kl = jnp.where(hi_half, 0.0, kk * jnp.exp(jnp.where(hi_half, 0.0, ref_l - g)))
            p = _dg(ql.astype(BF16), kl.astype(BF16), _NT)
            att = att + (p if same is None else jnp.where(same, p, 0.0))
        o = o + _dg(att.astype(BF16), v.astype(BF16), _NN)
        ke = kk * jnp.exp(g_last - g)
        st_ref[...] = st * jnp.exp(g_last) + _dg(v.T.astype(BF16), ke.astype(BF16), _NN)
        on = o * lax.rsqrt(jnp.mean(o * o, axis=-1, keepdims=True) + RMS_EPS) * gn
        o_ref[r, :] = (on * gt_ref[r, :]).astype(o_ref.dtype)
        return carry

    lax.fori_loop(0, n_chunks, body, 0)

    @pl.when(tb == pl.num_programs(2) - 1)
    def _():
        s_out_ref[...] = st_ref[...].T


def gla_prompt(proj, gnorm, s0, *, batch, seq, heads, block_t, chunk, out_dtype):
    m, n = proj.shape
    dk = n // 4 // heads
    nb = seq // block_t
    col = lambda part: (lambda b, h, t: (b * nb + t, part * heads + h))
    st_spec = pl.BlockSpec((None, None, dk, dk), lambda b, h, t: (b, h, 0, 0))
    return pl.pallas_call(
        functools.partial(_gla_kernel, chunk=chunk, n_chunks=block_t // chunk),
        out_shape=[jax.ShapeDtypeStruct((m, heads * dk), out_dtype),
                   jax.ShapeDtypeStruct((batch, heads, dk, dk), F32)],
        grid=(batch, heads, nb),
        in_specs=[pl.BlockSpec((block_t, dk), col(0)), pl.BlockSpec((block_t, dk), col(1)),
                  pl.BlockSpec((block_t, dk), col(2)), pl.BlockSpec((block_t, dk), col(3)),
                  pl.BlockSpec((1, dk), lambda b, h, t: (0, 0)), st_spec],
        out_specs=[pl.BlockSpec((block_t, dk), col(0)), st_spec],
        scratch_shapes=[pltpu.VMEM((dk, dk), F32)],
        compiler_params=_params("arbitrary", "arbitrary", "arbitrary"),
        name="gla_prompt",
    )(proj, proj, proj, proj, gnorm.reshape(1, dk), s0)


def _gla_step_kernel(p_ref, gn_ref, s0_ref, o_ref, s_out_ref, *, heads, dk):
    ri = lax.broadcasted_iota(jnp.int32, (dk, dk), 0)
    ci = lax.broadcasted_iota(jnp.int32, (dk, dk), 1)
    eye = ri == ci

    def to_col(rowv):
        return jnp.sum(jnp.where(eye, jnp.broadcast_to(rowv, (dk, dk)), 0.0), axis=-1, keepdims=True)

    gn = gn_ref[...]
    for h in range(heads):
        sl = lambda part: slice((part * heads + h) * dk, (part * heads + h + 1) * dk)
        q = p_ref[:, sl(0)]
        f = jnp.exp(p_ref[:, sl(1)])
        v = p_ref[:, sl(2)]
        gate = p_ref[:, sl(3)]
        s_new = to_col(f) * s0_ref[h] + to_col(1.0 - f) * v
        s_out_ref[h] = s_new
        o = jnp.sum(to_col(q) * s_new, axis=0, keepdims=True)
        on = o * lax.rsqrt(jnp.mean(o * o, axis=-1, keepdims=True) + RMS_EPS) * gn
        o_ref[:, h * dk:(h + 1) * dk] = on * gate


def gla_step(proj, gnorm, s0):
    b, n = proj.shape
    heads, dk = s0.shape[1], s0.shape[2]
    st_spec = pl.BlockSpec((None, heads, dk, dk), lambda i: (i, 0, 0, 0))
    o, s = pl.pallas_call(
        functools.partial(_gla_step_kernel, heads=heads, dk=dk),
        out_shape=[jax.ShapeDtypeStruct((b, 1, heads * dk), F32), jax.ShapeDtypeStruct(s0.shape, F32)],
        grid=(b,),
        in_specs=[pl.BlockSpec((None, 1, n), lambda i: (i, 0, 0)),
                  pl.BlockSpec((1, dk), lambda i: (0, 0)), st_spec],
        out_specs=[pl.BlockSpec((None, 1, heads * dk), lambda i: (i, 0, 0)), st_spec],
        compiler_params=_params("arbitrary"),
        name="gla_step",
    )(proj.reshape(b, 1, n), gnorm.reshape(1, dk), s0)
    return o.reshape(b, heads * dk), s


def _cumsum_kernel(x_ref, u_ref, o_ref, carry_ref):
    @pl.when(pl.program_id(1) == 0)
    def _():
        carry_ref[...] = jnp.zeros_like(carry_ref)

    tc = x_ref.shape[1]
    x = x_ref[...]
    hi, mid, lo = _split3(x)
    u = u_ref[...]
    f = carry_ref[:, :1] + (_dg(hi, u, _NN) + (_dg(mid, u, _NN) + _dg(lo, u, _NN)))
    o_ref[...] = f
    carry_ref[...] = jnp.broadcast_to(f[:, tc - 1:tc], carry_ref.shape)


def cumsum_lanes(x, *, tc):
    b, h, t = x.shape
    u = jnp.triu(jnp.ones((tc, tc), BF16))
    return pl.pallas_call(
        _cumsum_kernel,
        out_shape=jax.ShapeDtypeStruct(x.shape, F32),
        grid=(b, t // tc),
        in_specs=[pl.BlockSpec((None, h, tc), lambda i, j: (i, 0, j)),
                  pl.BlockSpec((tc, tc), lambda i, j: (0, 0))],
        out_specs=pl.BlockSpec((None, h, tc), lambda i, j: (i, 0, j)),
        scratch_shapes=[pltpu.VMEM((h, LANES), F32)],
        compiler_params=_params("arbitrary", "arbitrary"),
        name="cumsum_lanes",
    )(x, u)


def _fox_kernel(qi_tab, ki_tab, q_ref, k_ref, v_ref, fc_ref, fr_ref, o_ref, m_ref, l_ref, acc_ref, fcs_ref,
                *, tq, tk, hd):
    hp = pl.program_id(1)
    step = pl.program_id(2)
    qi = qi_tab[step]
    ki = ki_tab[step]
    heads_per_step = LANES // hd
    lane = lax.broadcasted_iota(jnp.int32, (1, LANES), 1)
    neg = jnp.float32(-jnp.inf)

    @pl.when(ki == 0)
    def _():
        m_ref[...] = jnp.full(m_ref.shape, neg, F32)
        l_ref[...] = jnp.zeros_like(l_ref)
        acc_ref[...] = jnp.zeros_like(acc_ref)
        fc = fc_ref[...]
        lane_h = lax.broadcasted_iota(jnp.int32, fc.shape, 1)
        for h in range(heads_per_step):
            col = jnp.sum(jnp.where(lane_h == hp * heads_per_step + h, fc, 0.0), axis=-1, keepdims=True)
            fcs_ref[h] = jnp.broadcast_to(col, (tq, LANES))

    q = q_ref[...]
    k = k_ref[...]
    v = v_ref[...]
    rows = qi * tq + lax.broadcasted_iota(jnp.int32, (tq, 1), 0)
    cols = ki * tk + lax.broadcasted_iota(jnp.int32, (1, tk), 1)
    causal = cols <= rows
    alpha_full = None
    pv_full = None
    for h in range(heads_per_step):
        in_head = jnp.logical_and(lane >= h * hd, lane < (h + 1) * hd)
        s = _dg(jnp.where(in_head, q, jnp.zeros_like(q)), k, _NT)
        s = s + fcs_ref[h][:, :1] - fr_ref[h:h + 1, :]
        s = jnp.where(causal, s, neg)
        m_prev = m_ref[h][:, :1]
        m_new = jnp.maximum(m_prev, jnp.max(s, axis=-1, keepdims=True))
        alpha = jnp.exp(m_prev - m_new)
        p = jnp.exp(s - m_new)
        l_new = alpha * l_ref[h][:, :1] + jnp.sum(p, axis=-1, keepdims=True)
        m_ref[h] = jnp.broadcast_to(m_new, (tq, LANES))
        l_ref[h] = jnp.broadcast_to(l_new, (tq, LANES))
        pv = _dg(p.astype(BF16), v, _NN)
        alpha_b = jnp.broadcast_to(alpha, (tq, LANES))
        alpha_full = alpha_b if alpha_full is None else jnp.where(in_head, alpha_b, alpha_full)
        pv_full = pv if pv_full is None else jnp.where(in_head, pv, pv_full)
    acc_ref[...] = alpha_full * acc_ref[...] + pv_full

    @pl.when(ki == qi)
    def _():
        l_full = l_ref[0]
        for h in range(1, heads_per_step):
            l_full = jnp.where(jnp.logical_and(lane >= h * hd, lane < (h + 1) * hd), l_ref[h], l_full)
        o_ref[...] = (acc_ref[...] / l_full).astype(o_ref.dtype)


def fox_prompt(q, k, v, f_col, f_row, *, batch, seq, hd, tq):
    m, n = q.shape
    nq = seq // tq
    hps = LANES // hd
    pairs = [(a, b) for a in range(nq) for b in range(a + 1)]
    qi_tab = jnp.asarray([p[0] for p in pairs], jnp.int32)
    ki_tab = jnp.asarray([p[1] for p in pairs], jnp.int32)
    n_h = f_col.shape[1]
    q_map = lambda b, hp, s, qt, kt: (b * nq + qt[s], hp)
    k_map = lambda b, hp, s, qt, kt: (b * nq + kt[s], hp)
    grid_spec = pltpu.PrefetchScalarGridSpec(
        num_scalar_prefetch=2,
        grid=(batch, n // LANES, len(pairs)),
        in_specs=[pl.BlockSpec((tq, LANES), q_map),
                  pl.BlockSpec((tq, LANES), k_map),
                  pl.BlockSpec((tq, LANES), k_map),
                  pl.BlockSpec((tq, n_h), lambda b, hp, s, qt, kt: (b * nq + qt[s], 0)),
                  pl.BlockSpec((None, None, hps, tq), lambda b, hp, s, qt, kt: (b, hp, 0, kt[s]))],
        out_specs=pl.BlockSpec((tq, LANES), q_map),
        scratch_shapes=[pltpu.VMEM((hps, tq, LANES), F32), pltpu.VMEM((hps, tq, LANES), F32),
                        pltpu.VMEM((tq, LANES), F32), pltpu.VMEM((hps, tq, LANES), F32)])
    return pl.pallas_call(
        functools.partial(_fox_kernel, tq=tq, tk=tq, hd=hd),
        out_shape=jax.ShapeDtypeStruct((m, n), BF16),
        grid_spec=grid_spec,
        compiler_params=_params("arbitrary", "arbitrary", "arbitrary"),
        name="fox_prompt",
    )(qi_tab, ki_tab, q, k, v, f_col, f_row)


def _fox_decode_kernel(pt_ref, *refs, pages_per_step, page, hd, heads):
    pps = pages_per_step
    q_ref, kn_ref, vn_ref, lfn_ref, u_ref = refs[:5]
    k_refs = refs[5:5 + pps]
    v_refs = refs[5 + pps:5 + 2 * pps]
    lf_ref = refs[5 + 2 * pps]
    o_ref = refs[6 + 2 * pps]
    m_ref, l_ref, acc_ref, fsum_ref = refs[7 + 2 * pps:]
    step = pl.program_id(1)
    d = heads * hd
    neg = jnp.float32(-jnp.inf)

    @pl.when(step == 0)
    def _():
        m_ref[...] = jnp.full(m_ref.shape, neg, F32)
        l_ref[...] = jnp.zeros_like(l_ref)
        acc_ref[...] = jnp.zeros_like(acc_ref)
        fsum_ref[...] = jnp.zeros_like(fsum_ref)

    lane_d = lax.broadcasted_iota(jnp.int32, (heads, d), 1)
    head_of_row = lax.broadcasted_iota(jnp.int32, (heads, d), 0)
    own = jnp.logical_and(lane_d >= head_of_row * hd, lane_d < (head_of_row + 1) * hd)
    q = q_ref[...]
    q_bd = jnp.where(own, jnp.broadcast_to(q, (heads, d)), 0.0)
    q_bd16 = q_bd.astype(BF16)
    u = u_ref[...]
    for pp in range(pps):
        kp = k_refs[pp][...].astype(BF16)
        vp = v_refs[pp][...].astype(BF16)
        lf = lf_ref[:, pp * page:(pp + 1) * page]
        hi, mid, lo = _split3(lf)
        f_past = fsum_ref[:, :1] + (_dg(hi, u, _NN) + (_dg(mid, u, _NN) + _dg(lo, u, _NN)))
        s = _dg(q_bd16, kp, _NT) - f_past
        m_prev = m_ref[:, :1]
        m_new = jnp.maximum(m_prev, jnp.max(s, axis=-1, keepdims=True))
        alpha = jnp.exp(m_prev - m_new)
        p = jnp.exp(s - m_new)
        l_ref[...] = jnp.broadcast_to(alpha * l_ref[:, :1] + jnp.sum(p, axis=-1, keepdims=True), l_ref.shape)
        m_ref[...] = jnp.broadcast_to(m_new, m_ref.shape)
        acc_ref[...] = alpha * acc_ref[...] + _dg(p.astype(BF16), vp, _NN)
        fsum_ref[...] = jnp.broadcast_to(f_past[:, page - 1:page], fsum_ref.shape)

    @pl.when(step == pl.num_programs(1) - 1)
    def _():
        lane_h = lax.broadcasted_iota(jnp.int32, (heads, LANES), 1)
        row_h = lax.broadcasted_iota(jnp.int32, (heads, LANES), 0)
        lfn = jnp.sum(jnp.where(lane_h == row_h, jnp.broadcast_to(lfn_ref[...], (heads, LANES)), 0.0),
                      axis=-1, keepdims=True)
        s_new = jnp.sum(q_bd * kn_ref[...], axis=-1, keepdims=True) - (fsum_ref[:, :1] + lfn)
        m_prev = m_ref[:, :1]
        m_new = jnp.maximum(m_prev, s_new)
        alpha = jnp.exp(m_prev - m_new)
        p_new = jnp.exp(s_new - m_new)
        l_new = alpha * l_ref[:, :1] + p_new
        acc = alpha * acc_ref[...] + p_new * vn_ref[...]
        o_ref[...] = jnp.sum(jnp.where(own, acc / l_new, 0.0), axis=0, keepdims=True)


def fox_decode(q, k_new, v_new, lf_new, cache_k, cache_v, lf_past_t, page_table, *, hd, pages_per_step):
    b, d = q.shape
    heads = d // hd
    n_pool, page, _ = cache_k.shape
    n_pages = page_table.shape[1]
    pps = pages_per_step
    u = jnp.triu(jnp.ones((page, page), BF16))
    row3 = lambda a: a.reshape(b, 1, a.shape[-1])
    row_spec = lambda w: pl.BlockSpec((None, 1, w), lambda i, s, pt: (i, 0, 0))
    page_spec = lambda pp: pl.BlockSpec((None, page, d), lambda i, s, pt: (pt[i, s * pps + pp], 0, 0))
    grid_spec = pltpu.PrefetchScalarGridSpec(
        num_scalar_prefetch=1,
        grid=(b, n_pages // pps),
        in_specs=[row_spec(d), row_spec(d), row_spec(d), row_spec(LANES),
                  pl.BlockSpec((page, page), lambda i, s, pt: (0, 0))]
                 + [page_spec(pp) for pp in range(pps)] + [page_spec(pp) for pp in range(pps)]
                 + [pl.BlockSpec((None, heads, pps * page), lambda i, s, pt: (i, 0, s))],
        out_specs=pl.BlockSpec((None, 1, d), lambda i, s, pt: (i, 0, 0)),
        scratch_shapes=[pltpu.VMEM((heads, LANES), F32), pltpu.VMEM((heads, LANES), F32),
                        pltpu.VMEM((heads, d), F32), pltpu.VMEM((heads, LANES), F32)])
    out = pl.pallas_call(
        functools.partial(_fox_decode_kernel, pages_per_step=pps, page=page, hd=hd, heads=heads),
        out_shape=jax.ShapeDtypeStruct((b, 1, d), F32),
        grid_spec=grid_spec,
        compiler_params=_params("arbitrary", "arbitrary"),
        name="fox_decode",
    )(page_table, row3(q), row3(k_new), row3(v_new), row3(lf_new), u,
      *([cache_k] * pps), *([cache_v] * pps), lf_past_t)
    return out.reshape(b, d)


def _moe_gu_kernel(te_ref, nv_ref, xs_ref, wg_ref, wu_ref, o_ref):
    i = pl.program_id(0)

    @pl.when(i < nv_ref[0])
    def _():
        x = xs_ref[...]
        g = _mm(x, wg_ref[...])
        u = _mm(x, wu_ref[...])
        o_ref[...] = (_silu(g) * u).astype(o_ref.dtype)

    @pl.when(i >= nv_ref[0])
    def _():
        o_ref[...] = jnp.zeros_like(o_ref)


def _moe_down_kernel(te_ref, nv_ref, a_ref, w_ref, o_ref):
    i = pl.program_id(0)

    @pl.when(i < nv_ref[0])
    def _():
        o_ref[...] = _mm(a_ref[...], w_ref[...])

    @pl.when(i >= nv_ref[0])
    def _():
        o_ref[...] = jnp.zeros_like(o_ref)


def moe_grouped(xs, tile_expert, n_valid, w_gu, w_d, *, tm, tn_gu, tn_d):
    p, d = xs.shape
    n_e, _, f2 = w_gu.shape
    f = f2 // 2
    nj = f // tn_gu
    n_tiles = p // tm
    col = lambda i, j, nv: jnp.where(i < nv[0], j, 0)
    gs_gu = pltpu.PrefetchScalarGridSpec(
        num_scalar_prefetch=2, grid=(n_tiles, nj),
        in_specs=[pl.BlockSpec((tm, d), lambda i, j, te, nv: (i, 0)),
                  pl.BlockSpec((None, d, tn_gu), lambda i, j, te, nv: (te[i], 0, col(i, j, nv))),
                  pl.BlockSpec((None, d, tn_gu), lambda i, j, te, nv: (te[i], 0, col(i, j, nv) + nj))],
        out_specs=pl.BlockSpec((tm, tn_gu), lambda i, j, te, nv: (i, j)))
    hs = pl.pallas_call(
        _moe_gu_kernel, out_shape=jax.ShapeDtypeStruct((p, f), BF16), grid_spec=gs_gu,
        compiler_params=_params("arbitrary", "arbitrary"), name="moe_gate_up",
    )(tile_expert, n_valid, xs, w_gu, w_gu)
    gs_d = pltpu.PrefetchScalarGridSpec(
        num_scalar_prefetch=2, grid=(n_tiles, d // tn_d),
        in_specs=[pl.BlockSpec((tm, f), lambda i, j, te, nv: (i, 0)),
                  pl.BlockSpec((None, f, tn_d), lambda i, j, te, nv: (te[i], 0, col(i, j, nv)))],
        out_specs=pl.BlockSpec((tm, tn_d), lambda i, j, te, nv: (i, j)))
    return pl.pallas_call(
        _moe_down_kernel, out_shape=jax.ShapeDtypeStruct((p, d), F32), grid_spec=gs_d,
        compiler_params=_params("arbitrary", "arbitrary"), name="moe_down",
    )(tile_expert, n_valid, hs, w_d)


def _combine_kernel(x_ref, g_ref, y0_ref, y1_ref, w_ref, o_ref):
    w = w_ref[...]
    y = w[:, 0:1] * y0_ref[...] + w[:, 1:2] * y1_ref[...]
    o_ref[...] = x_ref[...] + g_ref[...] * y


def moe_combine(x, gate, y0, y1, w01, *, tm, rows_per_group):
    m, d = x.shape
    row = lambda i, j: (i, 0)
    return pl.pallas_call(
        _combine_kernel,
        out_shape=jax.ShapeDtypeStruct((m, d), F32),
        grid=(m // tm, 1),
        in_specs=[pl.BlockSpec((tm, d), row), _mod_spec(gate, tm, rows_per_group),
                  pl.BlockSpec((tm, d), row), pl.BlockSpec((tm, d), row), pl.BlockSpec((tm, LANES), row)],
        out_specs=pl.BlockSpec((tm, d), row),
        compiler_params=_params("arbitrary", "arbitrary"),
        name="moe_combine",
    )(x, gate, y0, y1, w01)


def moe_prompt(x, gate, h, gate_full, ids, w_gu, w_d, *, tm, rows_per_group):
    m, d = h.shape
    n_e = w_gu.shape[0]
    top_k = 2
    e_flat = ids[:, :top_k].reshape(-1)
    onehot = (e_flat[:, None] == jnp.arange(n_e, dtype=jnp.int32)[None, :]).astype(jnp.int32)
    csum = jnp.cumsum(onehot, axis=0)
    rank = jnp.take_along_axis(csum, e_flat[:, None], axis=1)[:, 0] - 1
    counts = csum[-1]
    tiles_per_e = (counts + tm - 1) // tm
    tile_end = jnp.cumsum(tiles_per_e)
    dest = (tile_end - tiles_per_e)[e_flat] * tm + rank
    n_tiles = (top_k * m) // tm + n_e
    tile_expert = jnp.minimum(jnp.searchsorted(tile_end, jnp.arange(n_tiles, dtype=jnp.int32), side="right"),
                              n_e - 1).astype(jnp.int32)
    n_valid = tile_end[-1:].astype(jnp.int32)
    src = jnp.zeros((n_tiles * tm,), jnp.int32).at[dest].set(
        jnp.arange(top_k * m, dtype=jnp.int32) // top_k, unique_indices=True)
    xs = jnp.take(h, src, axis=0)
    ys = moe_grouped(xs, tile_expert, n_valid, w_gu, w_d, tm=tm, tn_gu=512, tn_d=min(512, d))
    pos = dest.reshape(m, top_k)
    y0 = jnp.take(ys, pos[:, 0], axis=0)
    y1 = jnp.take(ys, pos[:, 1], axis=0)
    gsel = jnp.take_along_axis(gate_full, ids[:, :top_k], axis=1)
    w01 = jnp.pad(gsel, ((0, 0), (0, LANES - top_k)))
    return moe_combine(x, gate, y0, y1, w01, tm=tm, rows_per_group=rows_per_group)


def _moe_dense_gu_kernel(h_ref, wg_ref, wu_ref, o_ref):
    h = h_ref[...]
    g = _mm(h, wg_ref[...], precise=True)
    u = _mm(h, wu_ref[...], precise=True)
    o_ref[...] = _silu(g) * u


def _moe_dense_down_kernel(a_ref, w_ref, gf_ref, x_ref, g_ref, o_ref, acc_ref):
    e = pl.program_id(1)

    @pl.when(e == 0)
    def _():
        acc_ref[...] = jnp.zeros_like(acc_ref)

    acc_ref[...] += gf_ref[:, :1] * _mm(a_ref[...], w_ref[...], precise=True)

    @pl.when(e == pl.num_programs(1) - 1)
    def _():
        o_ref[...] = x_ref[...] + g_ref[...] * acc_ref[...]


def moe_sample(x, gate, h, gate_full, w_gu, w_d, *, tn_gu, tn_d):
    m, d = h.shape
    n_e, _, f2 = w_gu.shape
    f = f2 // 2
    nj = f // tn_gu
    hs = pl.pallas_call(
        _moe_dense_gu_kernel,
        out_shape=jax.ShapeDtypeStruct((n_e, m, f), F32),
        grid=(n_e, nj),
        in_specs=[pl.BlockSpec((m, d), lambda e, j: (0, 0)),
                  pl.BlockSpec((None, d, tn_gu), lambda e, j: (e, 0, j)),
                  pl.BlockSpec((None, d, tn_gu), lambda e, j: (e, 0, j + nj))],
        out_specs=pl.BlockSpec((None, m, tn_gu), lambda e, j: (e, 0, j)),
        compiler_params=_params("arbitrary", "arbitrary"), name="moe_dense_gate_up",
    )(h, w_gu, w_gu)
    gf = jnp.broadcast_to(gate_full[:, :n_e].T[:, :, None], (n_e, m, LANES))
    gate2 = gate.reshape(m, d)
    return pl.pallas_call(
        _moe_dense_down_kernel,
        out_shape=jax.ShapeDtypeStruct((m, d), F32),
        grid=(d // tn_d, n_e),
        in_specs=[pl.BlockSpec((None, m, f), lambda j, e: (e, 0, 0)),
                  pl.BlockSpec((None, f, tn_d), lambda j, e: (e, 0, j)),
                  pl.BlockSpec((None, m, LANES), lambda j, e: (e, 0, 0)),
                  pl.BlockSpec((m, tn_d), lambda j, e: (0, j)),
                  pl.BlockSpec((m, tn_d), lambda j, e: (0, j))],
        out_specs=pl.BlockSpec((m, tn_d), lambda j, e: (0, j)),
        scratch_shapes=[pltpu.VMEM((m, tn_d), F32)],
        compiler_params=_params("arbitrary", "arbitrary"), name="moe_dense_down",
    )(hs, w_d, gf, x, gate2)


def _trunk(x, mods, kv_mods, out_mods, s0, past, p, *, batch, seq, tm, precise):
    m, d = x.shape
    depth = p["norm_mix"].shape[0]
    n_a = p["a_w_in"].shape[0]
    heads_a = s0.shape[2]
    hd = p["hd"]
    rpg = seq if mods[0][0].shape[1] == 1 else m
    act_dtype = F32 if precise else BF16
    common = dict(tm=tm, rows_per_group=rpg, precise=precise)
    states = []
    kv = None
    for layer in range(depth):
        sh1, sc1, g1, sh2, sc2, g2 = mods[layer]
        if layer == n_a:
            n_h = p["kv_w"].shape[1] // (2 * hd + 1)
            b_dim = n_h * hd
            w_f = jnp.pad(p["kv_w"][:, 2 * b_dim:], ((0, 0), (0, LANES - n_h)))
            b_f = jnp.pad(p["kv_bf"], (0, LANES - n_h)).reshape(1, LANES)
            k32, v32, k16, v16, lf = nm_shared_kv(
                x, p["kv_norm"], kv_mods[0], kv_mods[1], p["kv_w"][:, :b_dim], p["kv_w"][:, b_dim:2 * b_dim],
                w_f, b_f, tn=min(512, b_dim), **common)
            kv = (k32, v32, lf[:, :n_h])
            if past is None:
                lf_t = lf[:, :n_h].reshape(batch, seq, n_h).transpose(0, 2, 1)
                f_row = cumsum_lanes(lf_t, tc=min(512, seq))
                f_col = f_row.transpose(0, 2, 1).reshape(m, n_h)
                f_row = f_row.reshape(batch, n_h // (LANES // hd), LANES // hd, seq)
        if layer < n_a:
            proj = nm_hgrn2_proj(x, p["norm_mix"][layer], sh1, sc1, p["a_w_in"][layer], p["a_lb"], layer,
                                 tn=min(512, d), **common)
            if past is None:
                o, s = gla_prompt(proj, p["a_gnorm"][layer], s0[layer], batch=batch, seq=seq, heads=heads_a,
                                  block_t=min(1024, seq), chunk=min(128, seq), out_dtype=act_dtype)
            else:
                o, s = gla_step(proj, p["a_gnorm"][layer], s0[layer])
            states.append(s)
            x = matmul_residual(o, p["a_w_o"][layer], x, g1, tn=min(512, d), **common)
        else:
            jb = layer - n_a
            if past is None:
                q = nm_linear(x, p["norm_mix"][layer], sh1, sc1, p["b_w_q"][jb], out_scale=hd ** -0.5,
                              tn=min(512, d), out_dtype=BF16, **common)
                o = fox_prompt(q, k16, v16, f_col, f_row, batch=batch, seq=seq, hd=hd, tq=min(512, seq))
            else:
                q = nm_linear(x, p["norm_mix"][layer], sh1, sc1, p["b_w_q"][jb], out_scale=hd ** -0.5,
                              tn=min(512, d), out_dtype=F32, **common)
                o = fox_decode(q, k32, v32, lf, past[0], past[1], past[2], past[3], hd=hd,
                               pages_per_step=min(4, past[3].shape[1]))
            x = matmul_residual(o, p["b_w_o"][jb], x, g1, tn=min(512, d), **common)
        if layer % 2 == 0:
            w_gu = p["ffn_w_gu"][layer // 2]
            f = w_gu.shape[1] // 2
            tn = 256 if f % 256 == 0 else 128
            hmid = nm_swiglu(x, p["norm_ffn"][layer], sh2, sc2, w_gu, tn=tn, out_dtype=act_dtype, **common)
            x = matmul_residual(hmid, p["ffn_w_d"][layer // 2], x, g2, tn=min(512, d), **common)
        else:
            i_m = layer // 2
            h, gate_full, ids = nm_router(x, p["norm_ffn"][layer], sh2, sc2, p["moe_w_r"][i_m], p["moe_b_r"][i_m],
                                          h_dtype=act_dtype, **common)
            if past is None:
                x = moe_prompt(x, g2, h, gate_full, ids, p["moe_w_gu"][i_m], p["moe_w_d"][i_m],
                               tm=tm, rows_per_group=rpg)
            else:
                x = moe_sample(x, g2, h, gate_full, p["moe_w_gu"][i_m], p["moe_w_d"][i_m],
                               tn_gu=512, tn_d=min(512, d))
    y = final_norm_mod(x, p["final_norm"], out_mods[0], out_mods[1], tm=tm, rows_per_group=rpg)
    return y, jnp.stack(states), kv


def kernel(x_prompt, x_sample, c_prompt, c_sample, cache_k, cache_v, cache_logf, state_a, page_table, a_w_in, a_lb, a_gnorm, a_w_o, b_w_q, b_w_o, kv_norm, kv_ada_w, kv_ada_b, kv_w, kv_bf, norm_mix, norm_ffn, ada_w, ada_b, ffn_w_gu, ffn_w_d, moe_w_r, moe_b_r, moe_w_gu, moe_w_d, final_norm, out_ada_w, out_ada_b):
    b, t, d = x_prompt.shape
    db, dt, _ = x_sample.shape
    n_pool, page, n_h, hd = cache_k.shape
    n_a, _, heads_a, dk, dv = state_a.shape
    depth = norm_mix.shape[0]
    p = dict(a_w_in=a_w_in, a_lb=a_lb, a_gnorm=a_gnorm, a_w_o=a_w_o, b_w_q=b_w_q, b_w_o=b_w_o, kv_norm=kv_norm,
             kv_w=kv_w, kv_bf=kv_bf, norm_mix=norm_mix, norm_ffn=norm_ffn, ffn_w_gu=ffn_w_gu, ffn_w_d=ffn_w_d,
             moe_w_r=moe_w_r, moe_b_r=moe_b_r, moe_w_gu=moe_w_gu, moe_w_d=moe_w_d, final_norm=final_norm, hd=hd)

    c_all = jnp.concatenate([c_prompt, c_sample], axis=0)
    ada = cond_linear(c_all, ada_w, ada_b)
    kv_ada = cond_linear(c_all, kv_ada_w[None], kv_ada_b[None])[0]
    out_ada = cond_linear(c_all, out_ada_w[None], out_ada_b[None])[0]

    def split_prompt(a, n):
        return [a[:b, i * d:(i + 1) * d].reshape(b, 1, d) for i in range(n)]

    def split_sample(a, n):
        return [a[b:, i * d:(i + 1) * d].reshape(1, db * dt, d) for i in range(n)]

    mods_p = [split_prompt(ada[l], 6) for l in range(depth)]
    mods_s = [split_sample(ada[l], 6) for l in range(depth)]

    s0_prompt = jnp.zeros((n_a, b, heads_a, dk, dv), F32)
    y_p, st_p, kv_p = _trunk(x_prompt.reshape(b * t, d), mods_p, split_prompt(kv_ada, 2), split_prompt(out_ada, 2),
                             s0_prompt, None, p, batch=b, seq=t, tm=min(1024, t), precise=False)

    n_pages = page_table.shape[1]
    lf_past_t = cache_logf[page_table].reshape(db, n_pages * page, n_h).transpose(0, 2, 1)
    past = (cache_k.reshape(n_pool, page, n_h * hd), cache_v.reshape(n_pool, page, n_h * hd), lf_past_t, page_table)
    y_s, st_s, kv_s = _trunk(x_sample.reshape(db * dt, d), mods_s, split_sample(kv_ada, 2), split_sample(out_ada, 2),
                             state_a, past, p, batch=db, seq=dt, tm=db * dt, precise=True)

    return (y_p.reshape(b, t, d), y_s.reshape(db, dt, d), st_p,
            kv_p[0].reshape(b, t, n_h, hd), kv_p[1].reshape(b, t, n_h, hd), kv_p[2].reshape(b, t, n_h),
            st_s, kv_s[0].reshape(db, dt, n_h, hd), kv_s[1].reshape(db, dt, n_h, hd), kv_s[2].reshape(db, dt, n_h))
```
